```python
import jax, jax.numpy as jnp
from jax import lax
import numpy as np

D_MODEL = 2048
BATCH = 2
SEQ = 8192
DEPTH = 4

GRID_W = 64
CTX_LEN = 256
N_MIXERS = 2
N_CONV_LAYERS = (DEPTH + 1) // N_MIXERS
N_ATTN_LAYERS = DEPTH // N_MIXERS
CONV_WIDTH = 3
HEAD_DIM = 64
N_HEADS = D_MODEL // HEAD_DIM
N_KV_HEADS = 4
GROUP = N_HEADS // N_KV_HEADS
QKV_WIDTH = (N_HEADS + 2 * N_KV_HEADS) * HEAD_DIM
WINDOW = 128
BLOCK = 128
ROPE_BASE = 10000.0
D_FF = -(-8 * D_MODEL // (3 * 256)) * 256
N_MOD = 6
EPS = 1e-6
NEG_INF = -1e30

kernel_name = 'hybrid_conv_swa_dit_trunk'


def rmsnorm(x, g):
    xf = x.astype(jnp.float32)
    y = xf * lax.rsqrt(jnp.mean(xf * xf, axis=-1, keepdims=True) + EPS)
    return (y * g.astype(jnp.float32)).astype(x.dtype)


def modulate(h, shift, scale):
    return h * (1 + scale) + shift


def axial_rope_tables(rows, cols, dtype):
    n_freq = HEAD_DIM // 4
    inv = ROPE_BASE ** (-jnp.arange(n_freq, dtype=jnp.float32) / n_freq)
    ang_r = rows.astype(jnp.float32)[:, None] * inv
    ang_c = cols.astype(jnp.float32)[:, None] * inv
    return (jnp.cos(ang_r).astype(dtype), jnp.sin(ang_r).astype(dtype),
            jnp.cos(ang_c).astype(dtype), jnp.sin(ang_c).astype(dtype))


def rope_half(x, cos, sin):
    x1, x2 = jnp.split(x, 2, axis=-1)
    c = cos[:, None, :]
    s = sin[:, None, :]
    return jnp.concatenate([x1 * c - x2 * s, x2 * c + x1 * s], axis=-1)


def apply_axial_rope(x, rope):
    cos_r, sin_r, cos_c, sin_c = rope
    x_row, x_col = jnp.split(x, 2, axis=-1)
    return jnp.concatenate([rope_half(x_row, cos_r, sin_r), rope_half(x_col, cos_c, sin_c)], axis=-1)


def depthwise_conv3(u, w):
    return lax.conv_general_dilated(u, w[:, None, :], window_strides=(1,),
                                    padding=[(CONV_WIDTH // 2, CONV_WIDTH // 2)],
                                    dimension_numbers=('NWC', 'WIO', 'NWC'),
                                    feature_group_count=u.shape[-1])


def short_conv_mixer(h, w_in, k, w_out):
    b_gate, c_gate, v = jnp.split(h @ w_in, 3, axis=-1)
    return (b_gate * depthwise_conv3(c_gate * v, k)) @ w_out


def swiglu(h, w_gate, w_up, w_down):
    return (jax.nn.silu(h @ w_gate) * (h @ w_up)) @ w_down


def split_qkv(qkv):
    B, L, _ = qkv.shape
    q = qkv[..., :N_HEADS * HEAD_DIM].reshape(B, L, N_HEADS, HEAD_DIM)
    kv = qkv[..., N_HEADS * HEAD_DIM:].reshape(B, L, 2, N_KV_HEADS, HEAD_DIM)
    return q, kv[:, :, 0], kv[:, :, 1]


def latent_window_attention(q, k, v, k_ctx, v_ctx, sink):
    B, S = q.shape[:2]
    C = k_ctx.shape[1]
    nb = S // BLOCK
    scale = HEAD_DIM ** -0.5
    qb = q.reshape(B, nb, BLOCK, N_KV_HEADS, GROUP, HEAD_DIM).transpose(1, 0, 2, 3, 4, 5)

    def band(t):
        tp = jnp.pad(t, ((0, 0), (BLOCK, BLOCK), (0, 0), (0, 0)))
        tb = tp.reshape(B, nb + 2, BLOCK, N_KV_HEADS, HEAD_DIM)
        bd = jnp.concatenate([tb[:, :-2], tb[:, 1:-1], tb[:, 2:]], axis=2)
        return bd.transpose(1, 0, 2, 3, 4)

    kb, vb = band(k), band(v)
    sink_b = jnp.broadcast_to(sink.astype(jnp.float32).reshape(1, N_KV_HEADS, GROUP, 1, 1),
                              (B, N_KV_HEADS, GROUP, BLOCK, 1))
    r = jnp.arange(BLOCK)[:, None]
    s = jnp.arange(3 * BLOCK)[None, :]

    def one_block(args):
        blk, qi, ki, vi = args
        j = (blk - 1) * BLOCK + s
        valid = (jnp.abs(BLOCK + r - s) <= WINDOW) & (j >= 0) & (j < S)
        s_loc = jnp.einsum('bqkgd,bskd->bkgqs', qi, ki, preferred_element_type=jnp.float32) * scale
        s_loc = jnp.where(valid, s_loc, NEG_INF)
        s_ctx = jnp.einsum('bqkgd,bckd->bkgqc', qi, k_ctx, preferred_element_type=jnp.float32) * scale
        p = jax.nn.softmax(jnp.concatenate([s_loc, s_ctx, sink_b], axis=-1), axis=-1)
        p_loc = p[..., :3 * BLOCK].astype(vi.dtype)
        p_ctx = p[..., 3 * BLOCK:3 * BLOCK + C].astype(vi.dtype)
        o = (jnp.einsum('bkgqs,bskd->bqkgd', p_loc, vi)
             + jnp.einsum('bkgqc,bckd->bqkgd', p_ctx, v_ctx))
        return o.reshape(B, BLOCK, N_HEADS * HEAD_DIM)

    out = lax.map(one_block, (jnp.arange(nb), qb, kb, vb))
    return out.transpose(1, 0, 2, 3).reshape(B, S, N_HEADS * HEAD_DIM)


def context_attention(q_ctx, k_ctx, v_ctx, sink):
    B, C = q_ctx.shape[:2]
    scale = HEAD_DIM ** -0.5
    qg = q_ctx.reshape(B, C, N_KV_HEADS, GROUP, HEAD_DIM)
    sc = jnp.einsum('bqkgd,bckd->bkgqc', qg, k_ctx, preferred_element_type=jnp.float32) * scale
    sink_b = jnp.broadcast_to(sink.astype(jnp.float32).reshape(1, N_KV_HEADS, GROUP, 1, 1),
                              (B, N_KV_HEADS, GROUP, C, 1))
    p = jax.nn.softmax(jnp.concatenate([sc, sink_b], axis=-1), axis=-1)[..., :C].astype(v_ctx.dtype)
    o = jnp.einsum('bkgqc,bckd->bqkgd', p, v_ctx)
    return o.reshape(B, C, N_HEADS * HEAD_DIM)


def attention_mixer(h, h_ctx, w_qkv, w_o, sink, rope, need_ctx_out):
    q, k, v = split_qkv(h @ w_qkv)
    q = apply_axial_rope(q, rope)
    k = apply_axial_rope(k, rope)
    q_ctx, k_ctx, v_ctx = split_qkv(h_ctx @ w_qkv)
    y = latent_window_attention(q, k, v, k_ctx, v_ctx, sink) @ w_o
    y_ctx = context_attention(q_ctx, k_ctx, v_ctx, sink) @ w_o if need_ctx_out else None
    return y, y_ctx


def setup_inputs(seed: int = 0) -> dict:
    key = jax.random.key(seed)
    ks = jax.random.split(key, 16)
    f32 = jnp.float32
    D = D_MODEL

    def dense(k, shape, fan_in, gain=1.0):
        return jax.random.normal(k, shape, f32) * (gain * fan_in ** -0.5)

    return {
        'x': jax.random.normal(ks[0], (BATCH, SEQ, D), f32),
        'c': jax.random.normal(ks[1], (BATCH, D), f32),
        'ctx': jax.random.normal(ks[2], (BATCH, CTX_LEN, D), f32),
        'c_ctx': jax.random.normal(ks[3], (D,), f32),
        'w_mod': dense(ks[4], (DEPTH, D, N_MOD * D), D, 0.5),
        'b_mod': 0.02 * jax.random.normal(ks[5], (DEPTH, N_MOD * D), f32),
        'norm_g': 1.0 + 0.05 * jax.random.normal(ks[6], (DEPTH, 4, D), f32),
        'conv_w_in': dense(ks[7], (N_CONV_LAYERS, D, 3 * D), D),
        'conv_k': dense(ks[8], (N_CONV_LAYERS, CONV_WIDTH, D), CONV_WIDTH),
        'conv_w_out': dense(ks[9], (N_CONV_LAYERS, D, D), D),
        'attn_w_qkv': dense(ks[10], (N_ATTN_LAYERS, D, QKV_WIDTH), D),
        'attn_w_o': dense(ks[11], (N_ATTN_LAYERS, N_HEADS * HEAD_DIM, D), N_HEADS * HEAD_DIM),
        'attn_sink': 0.5 * jax.random.normal(ks[12], (N_ATTN_LAYERS, N_HEADS), f32),
        'ffn_w_gate': dense(ks[13], (DEPTH, D, D_FF), D),
        'ffn_w_up': dense(ks[14], (DEPTH, D, D_FF), D),
        'ffn_w_down': dense(ks[15], (DEPTH, D_FF, D), D_FF),
    }


def reference(x, c, ctx, c_ctx, w_mod, b_mod, norm_g, conv_w_in, conv_k, conv_w_out,
              attn_w_qkv, attn_w_o, attn_sink, ffn_w_gate, ffn_w_up, ffn_w_down):
    B, S, D = x.shape
    ROWS = S // GRID_W
    rows = jnp.repeat(jnp.arange(ROWS), GRID_W)
    cols = jnp.tile(jnp.arange(GRID_W), ROWS)
    rope = axial_rope_tables(rows, cols, x.dtype)
    silu_c = jax.nn.silu(c)
    silu_c_ctx = jax.nn.silu(c_ctx)

    for i in range(DEPTH):
        last = i == DEPTH - 1
        mod = (silu_c @ w_mod[i] + b_mod[i])[:, None, :]
        mod_ctx = (silu_c_ctx @ w_mod[i] + b_mod[i])[None, None, :]
        sh1, sc1, g1, sh2, sc2, g2 = jnp.split(mod, N_MOD, axis=-1)
        csh1, csc1, cg1, csh2, csc2, cg2 = jnp.split(mod_ctx, N_MOD, axis=-1)

        h = modulate(rmsnorm(x, norm_g[i, 0]), sh1, sc1)
        h_ctx = modulate(rmsnorm(ctx, norm_g[i, 0]), csh1, csc1)
        j = i // N_MIXERS
        if i % N_MIXERS == 0:
            y = short_conv_mixer(h, conv_w_in[j], conv_k[j], conv_w_out[j])
            y_ctx = short_conv_mixer(h_ctx, conv_w_in[j], conv_k[j], conv_w_out[j]) if not last else None
        else:
            y, y_ctx = attention_mixer(h, h_ctx, attn_w_qkv[j], attn_w_o[j], attn_sink[j], rope,
                                       need_ctx_out=not last)
        x = x + g1 * rmsnorm(y, norm_g[i, 1])

        h = modulate(rmsnorm(x, norm_g[i, 2]), sh2, sc2)
        x = x + g2 * rmsnorm(swiglu(h, ffn_w_gate[i], ffn_w_up[i], ffn_w_down[i]), norm_g[i, 3])

        if not last:
            ctx = ctx + cg1 * rmsnorm(y_ctx, norm_g[i, 1])
            h_ctx = modulate(rmsnorm(ctx, norm_g[i, 2]), csh2, csc2)
            ctx = ctx + cg2 * rmsnorm(swiglu(h_ctx, ffn_w_gate[i], ffn_w_up[i], ffn_w_down[i]), norm_g[i, 3])
    return x
```

```python
import functools

import jax
import jax.numpy as jnp
from jax import lax
from jax.experimental import pallas as pl
from jax.experimental.pallas import tpu as pltpu

F32 = jnp.float32
BF16 = jnp.bfloat16

D_MODEL = 2048
DEPTH = 4
GRID_W = 64
CONV_WIDTH = 3
HEAD_DIM = 64
N_HEADS = D_MODEL // HEAD_DIM
N_KV_HEADS = 4
GROUP = N_HEADS // N_KV_HEADS
KV_WIDTH = N_KV_HEADS * HEAD_DIM
QKV_WIDTH = (N_HEADS + 2 * N_KV_HEADS) * HEAD_DIM
WINDOW = 128
ROPE_BASE = 10000.0
N_MOD = 6
EPS = 1e-6
NEG_INF = -1e30

LANES = 128
BF16_SUBLANES = 16
VMEM_LIMIT_BYTES = 56 * 1024 * 1024
MOD_ROWS = 8

ROW_TILE = 512
HID_TILE = 512
QKV_TILE = 512
MOD_TILE = 1024
Q_TILE = 256
HALO = BF16_SUBLANES


def _params(*sem):
    return pltpu.CompilerParams(dimension_semantics=sem, vmem_limit_bytes=VMEM_LIMIT_BYTES)


def _rms(x, g):
    return x * lax.rsqrt(jnp.mean(x * x, axis=-1, keepdims=True) + EPS) * g


def _modulated(x, g, scale, shift):
    return (_rms(x, g) * (1.0 + scale) + shift).astype(BF16)


def _mod_body(c_ref, w_ref, b_ref, o_ref):
    c = c_ref[...]
    s = (c * jax.nn.sigmoid(c)).astype(BF16)
    o_ref[...] = jnp.dot(s, w_ref[...].astype(BF16), preferred_element_type=F32) + b_ref[...]


def _mod_call(cc, w_mod, b_mod):
    depth, d, width = w_mod.shape
    return pl.pallas_call(
        _mod_body,
        grid=(depth, width // MOD_TILE),
        in_specs=[
            pl.BlockSpec((MOD_ROWS, d), lambda l, j: (0, 0)),
            pl.BlockSpec((None, d, MOD_TILE), lambda l, j: (l, 0, j)),
            pl.BlockSpec((None, 1, MOD_TILE), lambda l, j: (l, 0, j)),
        ],
        out_specs=pl.BlockSpec((None, MOD_ROWS, MOD_TILE), lambda l, j: (l, 0, j)),
        out_shape=jax.ShapeDtypeStruct((depth, MOD_ROWS, width), F32),
        compiler_params=_params("parallel", "parallel"),
        name="mod_table",
    )(cc, w_mod, b_mod.reshape(depth, 1, width))


def _mod_spec(layer, part, mod_row):
    return pl.BlockSpec((None, 1, D_MODEL), lambda i, *_: (layer * MOD_ROWS + mod_row(i), 0, part))


def _gain_spec(layer, which):
    return pl.BlockSpec((None, 1, D_MODEL), lambda i, *_: (layer * 4 + which, 0, 0))


def _ffn_body(x_ref, sh_ref, sc_ref, gt_ref, gpre_ref, gpost_ref, wg_ref, wu_ref, wd_ref, o_ref, h_scr):
    j = pl.program_id(1)

    @pl.when(j == 0)
    def _():
        h_scr[...] = _modulated(x_ref[...], gpre_ref[...], sc_ref[...], sh_ref[...])
        o_ref[...] = jnp.zeros_like(o_ref)

    h = h_scr[...]
    g = jnp.dot(h, wg_ref[...], preferred_element_type=F32)
    u = jnp.dot(h, wu_ref[...], preferred_element_type=F32)
    a = (g * jax.nn.sigmoid(g) * u).astype(BF16)
    o_ref[...] += jnp.dot(a, wd_ref[...], preferred_element_type=F32)

    @pl.when(j == pl.num_programs(1) - 1)
    def _():
        o_ref[...] = x_ref[...] + gt_ref[...] * _rms(o_ref[...], gpost_ref[...])


def _ffn_call(x, mod3, gains, wg, wu, wd, *, layer, mod_row, tm):
    n, d = x.shape
    hid = wg.shape[1]
    return pl.pallas_call(
        _ffn_body,
        grid=(n // tm, hid // HID_TILE),
        in_specs=[
            pl.BlockSpec((tm, d), lambda i, j: (i, 0)),
            _mod_spec(layer, 3, mod_row), _mod_spec(layer, 4, mod_row), _mod_spec(layer, 5, mod_row),
            _gain_spec(layer, 2), _gain_spec(layer, 3),
            pl.BlockSpec((d, HID_TILE), lambda i, j: (0, j)),
            pl.BlockSpec((d, HID_TILE), lambda i, j: (0, j)),
            pl.BlockSpec((HID_TILE, d), lambda i, j: (j, 0)),
        ],
        out_specs=pl.BlockSpec((tm, d), lambda i, j: (i, 0)),
        out_shape=jax.ShapeDtypeStruct((n, d), F32),
        scratch_shapes=[pltpu.VMEM((tm, d), BF16)],
        compiler_params=_params("parallel", "arbitrary"),
        name="ffn",
    )(x, mod3, mod3, mod3, gains, gains, wg, wu, wd)


def _conv_body(xp_ref, x_ref, xn_ref, sh_ref, sc_ref, gt_ref, gpre_ref, gpost_ref,
               wb_ref, wc_ref, wv_ref, k_ref, wo_ref, o_ref, h_scr, *, tm, seq_len):
    i = pl.program_id(0)
    j = pl.program_id(1)

    @pl.when(j == 0)
    def _():
        def mod(xv):
            return _modulated(xv, gpre_ref[...], sc_ref[...], sh_ref[...])
        h_scr[0:HALO, :] = mod(xp_ref[...])
        h_scr[HALO:HALO + tm, :] = mod(x_ref[...])
        h_scr[HALO + tm:, :] = mod(xn_ref[...])
        o_ref[...] = jnp.zeros_like(o_ref)

    h_all = h_scr[...]
    c = jnp.dot(h_all, wc_ref[...], preferred_element_type=F32)
    v = jnp.dot(h_all, wv_ref[...], preferred_element_type=F32)
    b = jnp.dot(h_scr[HALO:HALO + tm, :], wb_ref[...], preferred_element_type=F32)
    cv = c * v
    row = lax.broadcasted_iota(jnp.int32, (tm, 1), 0)
    at_start = (i * tm) % seq_len == 0
    at_end = ((i + 1) * tm) % seq_len == 0
    prev = jnp.where((row == 0) & at_start, 0.0, cv[HALO - 1:HALO - 1 + tm])
    nxt = jnp.where((row == tm - 1) & at_end, 0.0, cv[HALO + 1:HALO + 1 + tm])
    k = k_ref[...]
    conv = k[0:1] * prev + k[1:2] * cv[HALO:HALO + tm] + k[2:3] * nxt
    a = (b * conv).astype(BF16)
    o_ref[...] += jnp.dot(a, wo_ref[...], preferred_element_type=F32)

    @pl.when(j == pl.num_programs(1) - 1)
    def _():
        o_ref[...] = x_ref[...] + gt_ref[...] * _rms(o_ref[...], gpost_ref[...])


def _conv_call(x, mod3, gains, w_in, k, w_out, *, layer, mod_row, tm, seq_len):
    n, d = x.shape
    th = HID_TILE
    nh = d // th
    halo_per_tile = tm // HALO
    last_halo = n // HALO - 1
    body = functools.partial(_conv_body, tm=tm, seq_len=seq_len)
    return pl.pallas_call(
        body,
        grid=(n // tm, nh),
        in_specs=[
            pl.BlockSpec((HALO, d), lambda i, j: (jnp.maximum(i * halo_per_tile - 1, 0), 0)),
            pl.BlockSpec((tm, d), lambda i, j: (i, 0)),
            pl.BlockSpec((HALO, d), lambda i, j: (jnp.minimum((i + 1) * halo_per_tile, last_halo), 0)),
            _mod_spec(layer, 0, mod_row), _mod_spec(layer, 1, mod_row), _mod_spec(layer, 2, mod_row),
            _gain_spec(layer, 0), _gain_spec(layer, 1),
            pl.BlockSpec((d, th), lambda i, j: (0, j)),
            pl.BlockSpec((d, th), lambda i, j: (0, nh + j)),
            pl.BlockSpec((d, th), lambda i, j: (0, 2 * nh + j)),
            pl.BlockSpec((CONV_WIDTH, th), lambda i, j: (0, j)),
            pl.BlockSpec((th, d), lambda i, j: (j, 0)),
        ],
        out_specs=pl.BlockSpec((tm, d), lambda i, j: (i, 0)),
        out_shape=jax.ShapeDtypeStruct((n, d), F32),
        scratch_shapes=[pltpu.VMEM((tm + 2 * HALO, d), BF16)],
        compiler_params=_params("parallel", "arbitrary"),
        name="conv_mixer",
    )(x, x, x, mod3, mod3, mod3, gains, gains, w_in, w_in, w_in, k, w_out)


def _rope(y, cos, sin):
    lane = lax.broadcasted_iota(jnp.int32, y.shape, 1)
    partner = jnp.where(lane % 32 < 16, pltpu.roll(y, LANES - 16, axis=1), pltpu.roll(y, 16, axis=1))
    return y * cos + partner * sin


def _qkv_body(x_ref, sh_ref, sc_ref, gpre_ref, w_ref, cos_ref, sin_ref, o_ref, h_scr, *, rope):
    j = pl.program_id(1)

    @pl.when(j == 0)
    def _():
        h_scr[...] = _modulated(x_ref[...], gpre_ref[...], sc_ref[...], sh_ref[...])

    y = jnp.dot(h_scr[...], w_ref[...], preferred_element_type=F32)
    n_q = (N_HEADS * HEAD_DIM) // QKV_TILE
    scale = HEAD_DIM ** -0.5

    def emit(n_rope_cols, mult):
        for cb in range(QKV_TILE // LANES):
            blk = y[:, cb * LANES:(cb + 1) * LANES]
            if rope and cb * LANES < n_rope_cols:
                blk = _rope(blk, cos_ref[...], sin_ref[...])
            if mult != 1.0:
                blk = blk * mult
            o_ref[:, cb * LANES:(cb + 1) * LANES] = blk.astype(o_ref.dtype)

    @pl.when(j < n_q)
    def _():
        emit(QKV_TILE, scale)

    @pl.when(j >= n_q)
    def _():
        emit(KV_WIDTH, 1.0)


def _qkv_call(x, mod3, gains, w_qkv, cos, sin, *, layer, mod_row, tm, rope, seq_len):
    n, d = x.shape
    assert QKV_WIDTH - N_HEADS * HEAD_DIM == QKV_TILE == 2 * KV_WIDTH
    pos_tiles = seq_len // tm
    body = functools.partial(_qkv_body, rope=rope)
    return pl.pallas_call(
        body,
        grid=(n // tm, QKV_WIDTH // QKV_TILE),
        in_specs=[
            pl.BlockSpec((tm, d), lambda i, j: (i, 0)),
            _mod_spec(layer, 0, mod_row), _mod_spec(layer, 1, mod_row),
            _gain_spec(layer, 0),
            pl.BlockSpec((d, QKV_TILE), lambda i, j: (0, j)),
            pl.BlockSpec((tm, LANES), lambda i, j: (i % pos_tiles, 0)),
            pl.BlockSpec((tm, LANES), lambda i, j: (i % pos_tiles, 0)),
        ],
        out_specs=pl.BlockSpec((tm, QKV_TILE), lambda i, j: (i, j)),
        out_shape=jax.ShapeDtypeStruct((n, QKV_WIDTH), BF16),
        scratch_shapes=[pltpu.VMEM((tm, d), BF16)],
        compiler_params=_params("parallel", "arbitrary"),
        name="qkv_proj",
    )(x, mod3, mod3, gains, w_qkv, cos, sin)


def _nt_dot(a, b):
    return lax.dot_general(a, b, (((1,), (1,)), ((), ())), preferred_element_type=F32)


def _head_attention(q, keys, values, masks, sink):
    scores = []
    for kk, mask in zip(keys, masks):
        s = _nt_dot(q, kk)
        if mask is not None:
            s = jnp.where(mask, s, NEG_INF)
        scores.append(s)
    m = sink
    for s in scores:
        m = jnp.maximum(m, jnp.max(s, axis=-1, keepdims=True))
    denom = jnp.exp(sink - m)
    out = None
    for s, vv in zip(scores, values):
        p = jnp.exp(s - m)
        denom = denom + jnp.sum(p, axis=-1, keepdims=True)
        pv = jnp.dot(p.astype(BF16), vv, preferred_element_type=F32)
        out = pv if out is None else out + pv
    return out / denom


def _attn_body(sink_ref, q_ref, kp_ref, km_ref, kn_ref, vp_ref, vm_ref, vn_ref, kc_ref, vc_ref,
               o_ref, k_scr, v_scr, *, tq, seq_len):
    t = pl.program_id(1)
    k_scr[0:WINDOW, :] = kp_ref[...]
    k_scr[WINDOW:WINDOW + tq, :] = km_ref[...]
    k_scr[WINDOW + tq:, :] = kn_ref[...]
    v_scr[0:WINDOW, :] = vp_ref[...]
    v_scr[WINDOW:WINDOW + tq, :] = vm_ref[...]
    v_scr[WINDOW + tq:, :] = vn_ref[...]

    r = lax.broadcasted_iota(jnp.int32, (WINDOW, 3 * WINDOW), 0)
    s = lax.broadcasted_iota(jnp.int32, (WINDOW, 3 * WINDOW), 1)
    band = jnp.abs(WINDOW + r - s) <= WINDOW
    for blk in range(tq // WINDOW):
        q0 = t * tq + blk * WINDOW
        key_pos = q0 - WINDOW + s
        mask = band & (key_pos >= 0) & (key_pos < seq_len)
        rows = slice(blk * WINDOW, (blk + 1) * WINDOW)
        band_rows = slice(blk * WINDOW, blk * WINDOW + 3 * WINDOW)
        for g in range(N_KV_HEADS):
            cols = slice(g * HEAD_DIM, (g + 1) * HEAD_DIM)
            kb, vb = k_scr[band_rows, cols], v_scr[band_rows, cols]
            kc, vc = kc_ref[:, cols], vc_ref[:, cols]
            for hh in range(GROUP):
                h = g * GROUP + hh
                hcols = slice(h * HEAD_DIM, (h + 1) * HEAD_DIM)
                o = _head_attention(q_ref[rows, hcols], (kb, kc), (vb, vc), (mask, None), sink_ref[h])
                o_ref[rows, hcols] = o.astype(o_ref.dtype)


def _attn_call(qkv, qkv_ctx, sink, *, batch, seq_len, ctx_len):
    n = qkv.shape[0]
    tq = Q_TILE
    q_tiles = seq_len // tq
    win_per_tile = tq // WINDOW
    win_per_seq = seq_len // WINDOW
    last_win = n // WINDOW - 1
    k_col = (N_HEADS * HEAD_DIM) // KV_WIDTH
    v_col = k_col + 1

    def prev_map(col):
        return lambda b, t: (jnp.maximum(b * win_per_seq + t * win_per_tile - 1, 0), col)

    def main_map(col):
        return lambda b, t: (b * q_tiles + t, col)

    def next_map(col):
        return lambda b, t: (jnp.minimum(b * win_per_seq + (t + 1) * win_per_tile, last_win), col)

    body = functools.partial(_attn_body, tq=tq, seq_len=seq_len)
    return pl.pallas_call(
        body,
        grid=(batch, q_tiles),
        in_specs=[
            pl.BlockSpec(memory_space=pltpu.SMEM),
            pl.BlockSpec((tq, N_HEADS * HEAD_DIM), main_map(0)),
            pl.BlockSpec((WINDOW, KV_WIDTH), prev_map(k_col)),
            pl.BlockSpec((tq, KV_WIDTH), main_map(k_col)),
            pl.BlockSpec((WINDOW, KV_WIDTH), next_map(k_col)),
            pl.BlockSpec((WINDOW, KV_WIDTH), prev_map(v_col)),
            pl.BlockSpec((tq, KV_WIDTH), main_map(v_col)),
            pl.BlockSpec((WINDOW, KV_WIDTH), next_map(v_col)),
            pl.BlockSpec((ctx_len, KV_WIDTH), lambda b, t: (b, k_col)),
            pl.BlockSpec((ctx_len, KV_WIDTH), lambda b, t: (b, v_col)),
        ],
        out_specs=pl.BlockSpec((tq, N_HEADS * HEAD_DIM), main_map(0)),
        out_shape=jax.ShapeDtypeStruct((n, N_HEADS * HEAD_DIM), BF16),
        scratch_shapes=[pltpu.VMEM((tq + 2 * WINDOW, KV_WIDTH), BF16),
                        pltpu.VMEM((tq + 2 * WINDOW, KV_WIDTH), BF16)],
        compiler_params=_params("parallel", "parallel"),
        name="window_attn",
    )(sink, qkv, qkv, qkv, qkv, qkv, qkv, qkv, qkv_ctx, qkv_ctx)


def _ctx_attn_body(sink_ref, q_ref, k_ref, v_ref, o_ref):
    for g in range(N_KV_HEADS):
        cols = slice(g * HEAD_DIM, (g + 1) * HEAD_DIM)
        kc, vc = k_ref[:, cols], v_ref[:, cols]
        for hh in range(GROUP):
            h = g * GROUP + hh
            hcols = slice(h * HEAD_DIM, (h + 1) * HEAD_DIM)
            o = _head_attention(q_ref[:, hcols], (kc,), (vc,), (None,), sink_ref[h])
            o_ref[:, hcols] = o.astype(o_ref.dtype)


def _ctx_attn_call(qkv_ctx, sink, *, batch, ctx_len):
    k_col = (N_HEADS * HEAD_DIM) // KV_WIDTH
    return pl.pallas_call(
        _ctx_attn_body,
        grid=(batch,),
        in_specs=[
            pl.BlockSpec(memory_space=pltpu.SMEM),
            pl.BlockSpec((ctx_len, N_HEADS * HEAD_DIM), lambda b: (b, 0)),
            pl.BlockSpec((ctx_len, KV_WIDTH), lambda b: (b, k_col)),
            pl.BlockSpec((ctx_len, KV_WIDTH), lambda b: (b, k_col + 1)),
        ],
        out_specs=pl.BlockSpec((ctx_len, N_HEADS * HEAD_DIM), lambda b: (b, 0)),
        out_shape=jax.ShapeDtypeStruct((batch * ctx_len, N_HEADS * HEAD_DIM), BF16),
        compiler_params=_params("parallel"),
        name="ctx_attn",
    )(sink, qkv_ctx, qkv_ctx, qkv_ctx)


def _oproj_body(a_ref, x_ref, gt_ref, gpost_ref, w_ref, o_ref):
    y = jnp.dot(a_ref[...], w_ref[...], preferred_element_type=F32)
    o_ref[...] = x_ref[...] + gt_ref[...] * _rms(y, gpost_ref[...])


def _oproj_call(a, x, mod3, gains, w_o, *, layer, mod_row, tm):
    n, d = x.shape
    return pl.pallas_call(
        _oproj_body,
        grid=(n // tm,),
        in_specs=[
            pl.BlockSpec((tm, a.shape[1]), lambda i: (i, 0)),
            pl.BlockSpec((tm, d), lambda i: (i, 0)),
            _mod_spec(layer, 2, mod_row),
            _gain_spec(layer, 1),
            pl.BlockSpec(w_o.shape, lambda i: (0, 0)),
        ],
        out_specs=pl.BlockSpec((tm, d), lambda i: (i, 0)),
        out_shape=jax.ShapeDtypeStruct((n, d), F32),
        compiler_params=_params("parallel"),
        name="attn_oproj",
    )(a, x, mod3, gains, w_o)


def _rope_tables(seq_len):
    t = jnp.arange(seq_len)
    n_freq = HEAD_DIM // 4
    inv = ROPE_BASE ** (-jnp.arange(n_freq, dtype=F32) / n_freq)
    ang_r = (t // GRID_W).astype(F32)[:, None] * inv
    ang_c = (t % GRID_W).astype(F32)[:, None] * inv
    cr, sr, cc, sc = jnp.cos(ang_r), jnp.sin(ang_r), jnp.cos(ang_c), jnp.sin(ang_c)
    cos = jnp.concatenate([cr, cr, cc, cc], axis=-1)
    sin = jnp.concatenate([-sr, sr, -sc, sc], axis=-1)
    reps = LANES // HEAD_DIM
    return jnp.tile(cos, (1, reps)), jnp.tile(sin, (1, reps))


def kernel(x, c, ctx, c_ctx, w_mod, b_mod, norm_g, conv_w_in, conv_k, conv_w_out,
           attn_w_qkv, attn_w_o, attn_sink, ffn_w_gate, ffn_w_up, ffn_w_down):
    batch, seq_len, d = x.shape
    ctx_len = ctx.shape[1]
    assert d == D_MODEL and norm_g.shape[0] == DEPTH
    assert seq_len % ROW_TILE == 0 and seq_len % Q_TILE == 0 and ctx_len % HALO == 0

    xf = x.reshape(batch * seq_len, d)
    cf = ctx.reshape(batch * ctx_len, d)

    cc = jnp.zeros((MOD_ROWS, d), F32).at[:batch].set(c).at[batch].set(c_ctx)
    mod3 = _mod_call(cc, w_mod, b_mod).reshape(DEPTH * MOD_ROWS, 1, N_MOD * d)
    gains = norm_g.reshape(DEPTH * 4, 1, d)
    cos, sin = _rope_tables(seq_len)

    x_tiles_per_seq = seq_len // ROW_TILE
    x_row = lambda i: i // x_tiles_per_seq
    c_row = lambda i: batch
    x_kw = dict(mod_row=x_row, tm=ROW_TILE)
    c_kw = dict(mod_row=c_row, tm=ctx_len)

    for i in range(DEPTH):
        last = i == DEPTH - 1
        j = i // 2
        if i % 2 == 0:
            w_in = conv_w_in[j].astype(BF16)
            w_out = conv_w_out[j].astype(BF16)
            xf_new = _conv_call(xf, mod3, gains, w_in, conv_k[j], w_out, layer=i, seq_len=seq_len, **x_kw)
            if not last:
                cf = _conv_call(cf, mod3, gains, w_in, conv_k[j], w_out, layer=i, seq_len=ctx_len, **c_kw)
            xf = xf_new
        else:
            w_qkv = attn_w_qkv[j].astype(BF16)
            w_o = attn_w_o[j].astype(BF16)
            qkv = _qkv_call(xf, mod3, gains, w_qkv, cos, sin, layer=i, rope=True, seq_len=seq_len, **x_kw)
            qkv_c = _qkv_call(cf, mod3, gains, w_qkv, cos, sin, layer=i, rope=False, seq_len=ctx_len, **c_kw)
            att = _attn_call(qkv, qkv_c, attn_sink[j], batch=batch, seq_len=seq_len, ctx_len=ctx_len)
            xf = _oproj_call(att, xf, mod3, gains, w_o, layer=i, **x_kw)
            if not last:
                att_c = _ctx_attn_call(qkv_c, attn_sink[j], batch=batch, ctx_len=ctx_len)
                cf = _oproj_call(att_c, cf, mod3, gains, w_o, layer=i, **c_kw)
        wg, wu, wd = ffn_w_gate[i].astype(BF16), ffn_w_up[i].astype(BF16), ffn_w_down[i].astype(BF16)
        xf = _ffn_call(xf, mod3, gains, wg, wu, wd, layer=i, **x_kw)
        if not last:
            cf = _ffn_call(cf, mod3, gains, wg, wu, wd, layer=i, **c_kw)
    return xf.reshape(batch, seq_len, d)
```

```python
import functools

import jax
import jax.numpy as jnp
from jax import lax
from jax.experimental import pallas as pl
from jax.experimental.pallas import tpu as pltpu

F32 = jnp.float32
BF16 = jnp.bfloat16

D_MODEL = 2048
DEPTH = 4
GRID_W = 64
CONV_WIDTH = 3
HEAD_DIM = 64
N_HEADS = D_MODEL // HEAD_DIM
N_KV_HEADS = 4
GROUP = N_HEADS // N_KV_HEADS
KV_WIDTH = N_KV_HEADS * HEAD_DIM
QKV_WIDTH = (N_HEADS + 2 * N_KV_HEADS) * HEAD_DIM
WINDOW = 128
ROPE_BASE = 10000.0
N_MOD = 6
EPS = 1e-6
NEG_INF = -1e30
LOG2E = 1.4426950408889634
Q_SCALE = HEAD_DIM ** -0.5 * LOG2E

LANES = 128
BF16_SUBLANES = 16
VMEM_LIMIT_BYTES = 56 * 1024 * 1024
MOD_ROWS = 8

ROW_TILE = 512
HID_TILE = 512
QKV_TILE = 512
MOD_TILE = 1024
Q_TILE = 256
HALO = BF16_SUBLANES


def _params(*sem):
    return pltpu.CompilerParams(dimension_semantics=sem, vmem_limit_bytes=VMEM_LIMIT_BYTES)


def _rms(x, g):
    return x * lax.rsqrt(jnp.mean(x * x, axis=-1, keepdims=True) + EPS) * g


def _modulated(x, g, scale, shift):
    return (_rms(x, g) * (1.0 + scale) + shift).astype(BF16)


def _mod_body(c_ref, w_ref, b_ref, o_ref):
    c = c_ref[...]
    s = (c * jax.nn.sigmoid(c)).astype(BF16)
    o_ref[...] = jnp.dot(s, w_ref[...].astype(BF16), preferred_element_type=F32) + b_ref[...]


def _mod_call(cc, w_mod, b_mod):
    depth, d, width = w_mod.shape
    return pl.pallas_call(
        _mod_body,
        grid=(depth, width // MOD_TILE),
        in_specs=[
            pl.BlockSpec((MOD_ROWS, d), lambda l, j: (0, 0)),
            pl.BlockSpec((None, d, MOD_TILE), lambda l, j: (l, 0, j)),
            pl.BlockSpec((None, 1, MOD_TILE), lambda l, j: (l, 0, j)),
        ],
        out_specs=pl.BlockSpec((None, MOD_ROWS, MOD_TILE), lambda l, j: (l, 0, j)),
        out_shape=jax.ShapeDtypeStruct((depth, MOD_ROWS, width), F32),
        compiler_params=_params("parallel", "parallel"),
        name="mod_table",
    )(cc, w_mod, b_mod.reshape(depth, 1, width))


def _mod_spec(layer, part, mod_row):
    return pl.BlockSpec((None, 1, D_MODEL), lambda i, *_: (layer * MOD_ROWS + mod_row(i), 0, part))


def _gain_spec(layer, which):
    return pl.BlockSpec((None, 1, D_MODEL), lambda i, *_: (layer * 4 + which, 0, 0))


def _ffn_body(x_ref, sh_ref, sc_ref, gt_ref, gpre_ref, gpost_ref, wg_ref, wu_ref, wd_ref, o_ref, h_scr):
    j = pl.program_id(1)

    @pl.when(j == 0)
    def _():
        h_scr[...] = _modulated(x_ref[...], gpre_ref[...], sc_ref[...], sh_ref[...])
        o_ref[...] = jnp.zeros_like(o_ref)

    h = h_scr[...]
    g = jnp.dot(h, wg_ref[...], preferred_element_type=F32)
    u = jnp.dot(h, wu_ref[...], preferred_element_type=F32)
    a = (g * jax.nn.sigmoid(g) * u).astype(BF16)
    o_ref[...] += jnp.dot(a, wd_ref[...], preferred_element_type=F32)

    @pl.when(j == pl.num_programs(1) - 1)
    def _():
        o_ref[...] = x_ref[...] + gt_ref[...] * _rms(o_ref[...], gpost_ref[...])


def _ffn_call(x, mod3, gains, wg, wu, wd, *, layer, mod_row, tm):
    n, d = x.shape
    hid = wg.shape[1]
    return pl.pallas_call(
        _ffn_body,
        grid=(n // tm, hid // HID_TILE),
        in_specs=[
            pl.BlockSpec((tm, d), lambda i, j: (i, 0)),
            _mod_spec(layer, 3, mod_row), _mod_spec(layer, 4, mod_row), _mod_spec(layer, 5, mod_row),
            _gain_spec(layer, 2), _gain_spec(layer, 3),
            pl.BlockSpec((d, HID_TILE), lambda i, j: (0, j)),
            pl.BlockSpec((d, HID_TILE), lambda i, j: (0, j)),
            pl.BlockSpec((HID_TILE, d), lambda i, j: (j, 0)),
        ],
        out_specs=pl.BlockSpec((tm, d), lambda i, j: (i, 0)),
        out_shape=jax.ShapeDtypeStruct((n, d), F32),
        scratch_shapes=[pltpu.VMEM((tm, d), BF16)],
        compiler_params=_params("parallel", "arbitrary"),
        name="ffn",
    )(x, mod3, mod3, mod3, gains, gains, wg, wu, wd)


def _conv_body(xp_ref, x_ref, xn_ref, sh_ref, sc_ref, gt_ref, gpre_ref, gpost_ref,
               wb_ref, wc_ref, wv_ref, k_ref, wo_ref, o_ref, h_scr, *, tm, seq_len):
    i = pl.program_id(0)
    j = pl.program_id(1)

    @pl.when(j == 0)
    def _():
        def mod(xv):
            return _modulated(xv, gpre_ref[...], sc_ref[...], sh_ref[...])
        h_scr[0:HALO, :] = mod(xp_ref[...])
        h_scr[HALO:HALO + tm, :] = mod(x_ref[...])
        h_scr[HALO + tm:, :] = mod(xn_ref[...])
        o_ref[...] = jnp.zeros_like(o_ref)

    h_all = h_scr[...]
    c = jnp.dot(h_all, wc_ref[...], preferred_element_type=F32)
    v = jnp.dot(h_all, wv_ref[...], preferred_element_type=F32)
    b = jnp.dot(h_scr[HALO:HALO + tm, :], wb_ref[...], preferred_element_type=F32)
    cv = c * v
    row = lax.broadcasted_iota(jnp.int32, (tm, 1), 0)
    at_start = (i * tm) % seq_len == 0
    at_end = ((i + 1) * tm) % seq_len == 0
    prev = jnp.where((row == 0) & at_start, 0.0, cv[HALO - 1:HALO - 1 + tm])
    nxt = jnp.where((row == tm - 1) & at_end, 0.0, cv[HALO + 1:HALO + 1 + tm])
    k = k_ref[...]
    conv = k[0:1] * prev + k[1:2] * cv[HALO:HALO + tm] + k[2:3] * nxt
    a = (b * conv).astype(BF16)
    o_ref[...] += jnp.dot(a, wo_ref[...], preferred_element_type=F32)

    @pl.when(j == pl.num_programs(1) - 1)
    def _():
        o_ref[...] = x_ref[...] + gt_ref[...] * _rms(o_ref[...], gpost_ref[...])


def _conv_call(x, mod3, gains, w_in, k, w_out, *, layer, mod_row, tm, seq_len):
    n, d = x.shape
    th = HID_TILE
    nh = d // th
    halo_per_tile = tm // HALO
    last_halo = n // HALO - 1
    body = functools.partial(_conv_body, tm=tm, seq_len=seq_len)
    return pl.pallas_call(
        body,
        grid=(n // tm, nh),
        in_specs=[
            pl.BlockSpec((HALO, d), lambda i, j: (jnp.maximum(i * halo_per_tile - 1, 0), 0)),
            pl.BlockSpec((tm, d), lambda i, j: (i, 0)),
            pl.BlockSpec((HALO, d), lambda i, j: (jnp.minimum((i + 1) * halo_per_tile, last_halo), 0)),
            _mod_spec(layer, 0, mod_row), _mod_spec(layer, 1, mod_row), _mod_spec(layer, 2, mod_row),
            _gain_spec(layer, 0), _gain_spec(layer, 1),
            pl.BlockSpec((d, th), lambda i, j: (0, j)),
            pl.BlockSpec((d, th), lambda i, j: (0, nh + j)),
            pl.BlockSpec((d, th), lambda i, j: (0, 2 * nh + j)),
            pl.BlockSpec((CONV_WIDTH, th), lambda i, j: (0, j)),
            pl.BlockSpec((th, d), lambda i, j: (j, 0)),
        ],
        out_specs=pl.BlockSpec((tm, d), lambda i, j: (i, 0)),
        out_shape=jax.ShapeDtypeStruct((n, d), F32),
        scratch_shapes=[pltpu.VMEM((tm + 2 * HALO, d), BF16)],
        compiler_params=_params("parallel", "arbitrary"),
        name="conv_mixer",
    )(x, x, x, mod3, mod3, mod3, gains, gains, w_in, w_in, w_in, k, w_out)


def _rope_lanes(y, cos, sin):
    lane = lax.broadcasted_iota(jnp.int32, y.shape, 1)
    partner = jnp.where(lane % 32 < 16, pltpu.roll(y, LANES - 16, axis=1), pltpu.roll(y, 16, axis=1))
    return y * cos + partner * sin


def _rope_rows(yt, cos_t, sin_t):
    q = HEAD_DIM // 4
    partner = jnp.concatenate([yt[q:2 * q], yt[0:q], yt[3 * q:4 * q], yt[2 * q:3 * q]], axis=0)
    return yt * cos_t + partner * sin_t


def _qkv_body(x_ref, sh_ref, sc_ref, gpre_ref, w_ref, cos_ref, sin_ref, cos_t_ref, sin_t_ref,
              qt_ref, k_ref, vt_ref, h_scr, *, rope):
    j = pl.program_id(1)

    @pl.when(j == 0)
    def _():
        h_scr[...] = _modulated(x_ref[...], gpre_ref[...], sc_ref[...], sh_ref[...])

    y = jnp.dot(h_scr[...], w_ref[...], preferred_element_type=F32)
    n_q = (N_HEADS * HEAD_DIM) // QKV_TILE

    @pl.when(j < n_q)
    def _():
        yt = y.T
        for hh in range(QKV_TILE // HEAD_DIM):
            rows = slice(hh * HEAD_DIM, (hh + 1) * HEAD_DIM)
            blk = yt[rows]
            if rope:
                blk = _rope_rows(blk, cos_t_ref[...], sin_t_ref[...])
            qt_ref[rows, :] = (blk * Q_SCALE).astype(qt_ref.dtype)

    @pl.when(j >= n_q)
    def _():
        for cb in range(KV_WIDTH // LANES):
            cols = slice(cb * LANES, (cb + 1) * LANES)
            blk = y[:, cols]
            if rope:
                blk = _rope_lanes(blk, cos_ref[...], sin_ref[...])
            k_ref[:, cols] = blk.astype(k_ref.dtype)
        vt_ref[...] = y[:, KV_WIDTH:].T.astype(vt_ref.dtype)


def _qkv_call(x, mod3, gains, w_qkv, rope_tabs, *, layer, mod_row, tm, rope, seq_len):
    n, d = x.shape
    hq = N_HEADS * HEAD_DIM
    assert QKV_WIDTH - hq == QKV_TILE == 2 * KV_WIDTH
    n_q = hq // QKV_TILE
    pos_tiles = seq_len // tm
    cos, sin, cos_t, sin_t = rope_tabs
    body = functools.partial(_qkv_body, rope=rope)
    return pl.pallas_call(
        body,
        grid=(n // tm, QKV_WIDTH // QKV_TILE),
        in_specs=[
            pl.BlockSpec((tm, d), lambda i, j: (i, 0)),
            _mod_spec(layer, 0, mod_row), _mod_spec(layer, 1, mod_row),
            _gain_spec(layer, 0),
            pl.BlockSpec((d, QKV_TILE), lambda i, j: (0, j)),
            pl.BlockSpec((tm, LANES), lambda i, j: (i % pos_tiles, 0)),
            pl.BlockSpec((tm, LANES), lambda i, j: (i % pos_tiles, 0)),
            pl.BlockSpec((HEAD_DIM, tm), lambda i, j: (0, i % pos_tiles)),
            pl.BlockSpec((HEAD_DIM, tm), lambda i, j: (0, i % pos_tiles)),
        ],
        out_specs=[
            pl.BlockSpec((QKV_TILE, tm), lambda i, j: (jnp.minimum(j, n_q - 1), i)),
            pl.BlockSpec((tm, KV_WIDTH), lambda i, j: (i, 0)),
            pl.BlockSpec((KV_WIDTH, tm), lambda i, j: (0, i)),
        ],
        out_shape=[
            jax.ShapeDtypeStruct((hq, n), BF16),
            jax.ShapeDtypeStruct((n, KV_WIDTH), BF16),
            jax.ShapeDtypeStruct((KV_WIDTH, n), BF16),
        ],
        scratch_shapes=[pltpu.VMEM((tm, d), BF16)],
        compiler_params=_params("parallel", "arbitrary"),
        name="qkv_proj",
    )(x, mod3, mod3, gains, w_qkv, cos, sin, cos_t, sin_t)


def _head_scores_t(qt, keys, masks):
    scores = []
    for kk, mask in zip(keys, masks):
        s = jnp.dot(kk, qt, preferred_element_type=F32)
        if mask is not None:
            s = jnp.where(mask, s, NEG_INF)
        scores.append(s)
    return scores


def _head_output_t(scores, values_t, sink2):
    m = sink2
    for s in scores:
        m = jnp.maximum(m, jnp.max(s, axis=0, keepdims=True))
    denom = jnp.exp2(sink2 - m)
    out = None
    for s, vt in zip(scores, values_t):
        p = jnp.exp2(s - m)
        denom = denom + jnp.sum(p, axis=0, keepdims=True)
        pv = jnp.dot(vt, p.astype(BF16), preferred_element_type=F32)
        out = pv if out is None else out + pv
    return out / denom


def _group_scores(q_ref, q_cols, g, keys, masks):
    return [_head_scores_t(q_ref[h * HEAD_DIM:(h + 1) * HEAD_DIM, q_cols], keys, masks)
            for h in range(g * GROUP, (g + 1) * GROUP)]


def _group_outputs(scores, o_ref, o_rows, g, values_t, sink_ref):
    h0 = g * GROUP
    outs = [_head_output_t(s, values_t, sink_ref[h0 + e] * LOG2E) for e, s in enumerate(scores)]
    for pair in range(GROUP // 2):
        o_pair = jnp.concatenate(outs[2 * pair:2 * pair + 2], axis=0).T
        lanes = slice((h0 + 2 * pair) * HEAD_DIM, (h0 + 2 * pair + 2) * HEAD_DIM)
        o_ref[o_rows, lanes] = o_pair.astype(o_ref.dtype)


def _attend_units(units, q_ref, o_ref, sink_ref):
    def scores_of(u):
        q_cols, g, keys, _, masks = u
        return _group_scores(q_ref, q_cols, g, keys, masks)

    pending = scores_of(units[0])
    for idx, (q_cols, g, _, values_t, _) in enumerate(units):
        ahead = scores_of(units[idx + 1]) if idx + 1 < len(units) else None
        _group_outputs(pending, o_ref, q_cols, g, values_t, sink_ref)
        pending = ahead


def _attn_body(sink_ref, q_ref, kp_ref, km_ref, kn_ref, vp_ref, vm_ref, vn_ref, kc_ref, vc_ref,
               o_ref, k_scr, v_scr, *, tq, seq_len):
    t = pl.program_id(1)
    k_scr[0:WINDOW, :] = kp_ref[...]
    k_scr[WINDOW:WINDOW + tq, :] = km_ref[...]
    k_scr[WINDOW + tq:, :] = kn_ref[...]
    v_scr[:, 0:WINDOW] = vp_ref[...]
    v_scr[:, WINDOW:WINDOW + tq] = vm_ref[...]
    v_scr[:, WINDOW + tq:] = vn_ref[...]

    kk = lax.broadcasted_iota(jnp.int32, (WINDOW, WINDOW), 0)
    qq = lax.broadcasted_iota(jnp.int32, (WINDOW, WINDOW), 1)
    units = []
    for blk in range(tq // WINDOW):
        q0 = t * tq + blk * WINDOW
        prev_ok = (kk >= qq) & (q0 >= WINDOW)
        next_ok = (kk <= qq) & (q0 + 2 * WINDOW <= seq_len)
        q_cols = slice(blk * WINDOW, (blk + 1) * WINDOW)
        band = [slice((blk + b) * WINDOW, (blk + b + 1) * WINDOW) for b in range(3)]
        for g in range(N_KV_HEADS):
            cols = slice(g * HEAD_DIM, (g + 1) * HEAD_DIM)
            keys = [k_scr[b, cols] for b in band] + [kc_ref[:, cols]]
            values_t = [v_scr[cols, b] for b in band] + [vc_ref[cols, :]]
            units.append((q_cols, g, keys, values_t, (prev_ok, None, next_ok, None)))
    _attend_units(units, q_ref, o_ref, sink_ref)


def _attn_call(qt, k, vt, k_ctx, vt_ctx, sink, *, batch, seq_len, ctx_len):
    hq, n = qt.shape
    tq = Q_TILE
    q_tiles = seq_len // tq
    win_per_tile = tq // WINDOW
    win_per_seq = seq_len // WINDOW
    last_win = n // WINDOW - 1

    def prev_win(b, t):
        return jnp.maximum(b * win_per_seq + t * win_per_tile - 1, 0)

    def next_win(b, t):
        return jnp.minimum(b * win_per_seq + (t + 1) * win_per_tile, last_win)

    body = functools.partial(_attn_body, tq=tq, seq_len=seq_len)
    return pl.pallas_call(
        body,
        grid=(batch, q_tiles),
        in_specs=[
            pl.BlockSpec(memory_space=pltpu.SMEM),
            pl.BlockSpec((hq, tq), lambda b, t: (0, b * q_tiles + t)),
            pl.BlockSpec((WINDOW, KV_WIDTH), lambda b, t: (prev_win(b, t), 0)),
            pl.BlockSpec((tq, KV_WIDTH), lambda b, t: (b * q_tiles + t, 0)),
            pl.BlockSpec((WINDOW, KV_WIDTH), lambda b, t: (next_win(b, t), 0)),
            pl.BlockSpec((KV_WIDTH, WINDOW), lambda b, t: (0, prev_win(b, t))),
            pl.BlockSpec((KV_WIDTH, tq), lambda b, t: (0, b * q_tiles + t)),
            pl.BlockSpec((KV_WIDTH, WINDOW), lambda b, t: (0, next_win(b, t))),
            pl.BlockSpec((ctx_len, KV_WIDTH), lambda b, t: (b, 0)),
            pl.BlockSpec((KV_WIDTH, ctx_len), lambda b, t: (0, b)),
        ],
        out_specs=pl.BlockSpec((tq, hq), lambda b, t: (b * q_tiles + t, 0)),
        out_shape=jax.ShapeDtypeStruct((n, hq), BF16),
        scratch_shapes=[pltpu.VMEM((tq + 2 * WINDOW, KV_WIDTH), BF16),
                        pltpu.VMEM((KV_WIDTH, tq + 2 * WINDOW), BF16)],
        compiler_params=_params("parallel", "parallel"),
        name="window_attn",
    )(sink, qt, k, k, k, vt, vt, vt, k_ctx, vt_ctx)


def _ctx_attn_body(sink_ref, q_ref, k_ref, vt_ref, o_ref):
    units = []
    for g in range(N_KV_HEADS):
        cols = slice(g * HEAD_DIM, (g + 1) * HEAD_DIM)
        units.append((slice(None), g, [k_ref[:, cols]], [vt_ref[cols, :]], (None,)))
    _attend_units(units, q_ref, o_ref, sink_ref)


def _ctx_attn_call(qt_ctx, k_ctx, vt_ctx, sink, *, batch, ctx_len):
    hq = qt_ctx.shape[0]
    return pl.pallas_call(
        _ctx_attn_body,
        grid=(batch,),
        in_specs=[
            pl.BlockSpec(memory_space=pltpu.SMEM),
            pl.BlockSpec((hq, ctx_len), lambda b: (0, b)),
            pl.BlockSpec((ctx_len, KV_WIDTH), lambda b: (b, 0)),
            pl.BlockSpec((KV_WIDTH, ctx_len), lambda b: (0, b)),
        ],
        out_specs=pl.BlockSpec((ctx_len, hq), lambda b: (b, 0)),
        out_shape=jax.ShapeDtypeStruct((batch * ctx_len, hq), BF16),
        compiler_params=_params("parallel"),
        name="ctx_attn",
    )(sink, qt_ctx, k_ctx, vt_ctx)


def _oproj_body(a_ref, x_ref, gt_ref, gpost_ref, w_ref, o_ref):
    y = jnp.dot(a_ref[...], w_ref[...], preferred_element_type=F32)
    o_ref[...] = x_ref[...] + gt_ref[...] * _rms(y, gpost_ref[...])


def _oproj_call(a, x, mod3, gains, w_o, *, layer, mod_row, tm):
    n, d = x.shape
    return pl.pallas_call(
        _oproj_body,
        grid=(n // tm,),
        in_specs=[
            pl.BlockSpec((tm, a.shape[1]), lambda i: (i, 0)),
            pl.BlockSpec((tm, d), lambda i: (i, 0)),
            _mod_spec(layer, 2, mod_row),
            _gain_spec(layer, 1),
            pl.BlockSpec(w_o.shape, lambda i: (0, 0)),
        ],
        out_specs=pl.BlockSpec((tm, d), lambda i: (i, 0)),
        out_shape=jax.ShapeDtypeStruct((n, d), F32),
        compiler_params=_params("parallel"),
        name="attn_oproj",
    )(a, x, mod3, gains, w_o)


def _rope_tables(seq_len):
    t = jnp.arange(seq_len)
    n_freq = HEAD_DIM // 4
    inv = ROPE_BASE ** (-jnp.arange(n_freq, dtype=F32) / n_freq)
    ang_r = (t // GRID_W).astype(F32)[:, None] * inv
    ang_c = (t % GRID_W).astype(F32)[:, None] * inv
    cr, sr, cc, sc = jnp.cos(ang_r), jnp.sin(ang_r), jnp.cos(ang_c), jnp.sin(ang_c)
    cos = jnp.concatenate([cr, cr, cc, cc], axis=-1)
    sin = jnp.concatenate([-sr, sr, -sc, sc], axis=-1)
    reps = LANES // HEAD_DIM
    return jnp.tile(cos, (1, reps)), jnp.tile(sin, (1, reps)), cos.T, sin.T


def kernel(x, c, ctx, c_ctx, w_mod, b_mod, norm_g, conv_w_in, conv_k, conv_w_out,
           attn_w_qkv, attn_w_o, attn_sink, ffn_w_gate, ffn_w_up, ffn_w_down):
    batch, seq_len, d = x.shape
    ctx_len = ctx.shape[1]
    assert d == D_MODEL and norm_g.shape[0] == DEPTH
    assert seq_len % ROW_TILE == 0 and seq_len % Q_TILE == 0 and ctx_len % HALO == 0

    xf = x.reshape(batch * seq_len, d)
    cf = ctx.reshape(batch * ctx_len, d)

    cc = jnp.zeros((MOD_ROWS, d), F32).at[:batch].set(c).at[batch].set(c_ctx)
    mod3 = _mod_call(cc, w_mod, b_mod).reshape(DEPTH * MOD_ROWS, 1, N_MOD * d)
    gains = norm_g.reshape(DEPTH * 4, 1, d)
    rope_tabs = _rope_tables(seq_len)

    x_tiles_per_seq = seq_len // ROW_TILE
    x_row = lambda i: i // x_tiles_per_seq
    c_row = lambda i: batch
    x_kw = dict(mod_row=x_row, tm=ROW_TILE)
    c_kw = dict(mod_row=c_row, tm=ctx_len)

    for i in range(DEPTH):
        last = i == DEPTH - 1
        j = i // 2
        if i % 2 == 0:
            w_in = conv_w_in[j].astype(BF16)
            w_out = conv_w_out[j].astype(BF16)
            xf_new = _conv_call(xf, mod3, gains, w_in, conv_k[j], w_out, layer=i, seq_len=seq_len, **x_kw)
            if not last:
                cf = _conv_call(cf, mod3, gains, w_in, conv_k[j], w_out, layer=i, seq_len=ctx_len, **c_kw)
            xf = xf_new
        else:
            w_qkv = attn_w_qkv[j].astype(BF16)
            w_o = attn_w_o[j].astype(BF16)
            qt, k, vt = _qkv_call(xf, mod3, gains, w_qkv, rope_tabs, layer=i, rope=True, seq_len=seq_len, **x_kw)
            qt_c, k_c, vt_c = _qkv_call(cf, mod3, gains, w_qkv, rope_tabs, layer=i, rope=False, seq_len=ctx_len,
                                        **c_kw)
            att = _attn_call(qt, k, vt, k_c, vt_c, attn_sink[j], batch=batch, seq_len=seq_len, ctx_len=ctx_len)
            xf = _oproj_call(att, xf, mod3, gains, w_o, layer=i, **x_kw)
            if not last:
                att_c = _ctx_attn_call(qt_c, k_c, vt_c, attn_sink[j], batch=batch, ctx_len=ctx_len)
                cf = _oproj_call(att_c, cf, mod3, gains, w_o, layer=i, **c_kw)
        wg, wu, wd = ffn_w_gate[i].astype(BF16), ffn_w_up[i].astype(BF16), ffn_w_down[i].astype(BF16)
        xf = _ffn_call(xf, mod3, gains, wg, wu, wd, layer=i, **x_kw)
        if not last:
            cf = _ffn_call(cf, mod3, gains, wg, wu, wd, layer=i, **c_kw)
    return xf.reshape(batch, seq_len, d)
```
